```python
import math
import jax
import jax.numpy as jnp
from jax import lax
import numpy as np

D_MODEL = 2048
BATCH = 16
SEQ = 256
DEPTH = 4
DEC_BATCH = 2
DEC_SEQ = 2048
PAST_LEN = 512

GRID_W = 64
N_MOD = 6
EPS = 1e-6
NEG_INF = -1e30

ATTN_Q_HEADS = 8
ATTN_KV_HEADS = 2
ATTN_HD = 64
ATTN_GROUP = ATTN_Q_HEADS // ATTN_KV_HEADS
ATTN_BLOCK = 128
WINDOW = 128
ROPE_BASE = 10000.0

SSD_HEADS = 12
SSD_HD = 64
SSD_INNER = SSD_HEADS * SSD_HD
SSD_GROUPS = 2
SSD_STATE = 128
SSD_CHUNK = 128

DN_HEADS = 6
DN_DK = 128
DN_DV = 128
DN_CHUNK = 64

CONV_K = 5

N_EXPERTS = 16
EXPERT_FF = 1024
EC_CAPACITY_FACTOR = 2

ATTN_COLS = (ATTN_Q_HEADS + 2 * ATTN_KV_HEADS) * ATTN_HD
SSD_XBC = SSD_INNER + 2 * SSD_GROUPS * SSD_STATE
SSD_COLS = SSD_INNER + SSD_XBC + 2 * SSD_HEADS
DN_QKV = DN_HEADS * (2 * DN_DK + DN_DV)
DN_COLS = DN_QKV + DN_HEADS * DN_DV + 4 * DN_HEADS
IN_COLS = ATTN_COLS + SSD_COLS + DN_COLS
MIX_WIDTH = ATTN_Q_HEADS * ATTN_HD + SSD_INNER + DN_HEADS * DN_DV

kernel_name = 'hybrid_diffusion_prefix_trunk_step'


def _split(a, widths):
    cuts = [int(v) for v in np.cumsum(widths)[:-1]]
    return jnp.split(a, cuts, axis=-1)


def _rmsnorm(x, w):
    xf = x.astype(jnp.float32)
    y = xf * lax.rsqrt(jnp.mean(xf * xf, axis=-1, keepdims=True) + EPS)
    return (y * w.astype(jnp.float32)).astype(x.dtype)


def _l2norm(x):
    return x * lax.rsqrt(jnp.sum(x * x, axis=-1, keepdims=True) + EPS)


def _flip(t):
    return jnp.flip(t, axis=1)


def _dwconv_centred(x, w):
    return lax.conv_general_dilated(
        x, w[:, None, :].astype(x.dtype), window_strides=(1,),
        padding=[(CONV_K // 2, CONV_K // 2)],
        dimension_numbers=('NWC', 'WIO', 'NWC'),
        feature_group_count=x.shape[-1])


def _grid_positions(n_tokens):
    rows = n_tokens // GRID_W
    r, col = jnp.meshgrid(jnp.arange(rows), jnp.arange(GRID_W), indexing='ij')
    return r.reshape(-1), col.reshape(-1)


def _rope_1d(x, pos):
    half = x.shape[-1] // 2
    inv = ROPE_BASE ** (-jnp.arange(half, dtype=jnp.float32) / half)
    ang = pos.astype(jnp.float32)[:, None] * inv[None, :]
    cos = jnp.cos(ang)[:, None, :]
    sin = jnp.sin(ang)[:, None, :]
    x1 = x[..., :half].astype(jnp.float32)
    x2 = x[..., half:].astype(jnp.float32)
    return jnp.concatenate([x1 * cos - x2 * sin, x2 * cos + x1 * sin], axis=-1)


def _rope_2d(x, row, col):
    h = x.shape[-1] // 2
    return jnp.concatenate([_rope_1d(x[..., :h], row), _rope_1d(x[..., h:], col)], axis=-1).astype(x.dtype)


def _tril_decay(a_cum):
    q = a_cum.shape[-1]
    tril = jnp.tril(jnp.ones((q, q), dtype=bool))
    diff = a_cum[..., :, None] - a_cum[..., None, :]
    return jnp.where(tril, jnp.exp(jnp.where(tril, diff, 0.0)), 0.0)


def _context_attention(q, k, v, sink):
    bsz, n_ctx = q.shape[:2]
    nqb = n_ctx // ATTN_BLOCK
    scale = ATTN_HD ** -0.5
    qb = jnp.moveaxis(q.reshape(bsz, nqb, ATTN_BLOCK, ATTN_KV_HEADS, ATTN_GROUP, ATTN_HD), 1, 0)
    sink_logit = jnp.broadcast_to(sink.astype(jnp.float32).reshape(1, ATTN_KV_HEADS, ATTN_GROUP, 1, 1),
                                  (bsz, ATTN_KV_HEADS, ATTN_GROUP, ATTN_BLOCK, 1))

    def one_block(qi):
        s = jnp.einsum('bqkgd,bskd->bkgqs', qi, k, preferred_element_type=jnp.float32) * scale
        p = jax.nn.softmax(jnp.concatenate([s, sink_logit], axis=-1), axis=-1)[..., :-1]
        return jnp.einsum('bkgqs,bskd->bqkgd', p.astype(v.dtype), v)

    o = lax.map(one_block, qb)
    return jnp.moveaxis(o, 0, 1).reshape(bsz, n_ctx, ATTN_Q_HEADS * ATTN_HD)


def _latent_attention(q, k, v, k_ctx, v_ctx, sink):
    bsz, n_lat = q.shape[:2]
    n_ctx = k_ctx.shape[1]
    w = ATTN_BLOCK
    nb = n_lat // w
    scale = ATTN_HD ** -0.5
    qb = q.reshape(bsz, nb, w, ATTN_KV_HEADS, ATTN_GROUP, ATTN_HD)
    pad = ((0, 0), (w, w), (0, 0), (0, 0))
    kp = jnp.pad(k, pad).reshape(bsz, nb + 2, w, ATTN_KV_HEADS, ATTN_HD)
    vp = jnp.pad(v, pad).reshape(bsz, nb + 2, w, ATTN_KV_HEADS, ATTN_HD)
    k_win = jnp.concatenate([kp[:, :-2], kp[:, 1:-1], kp[:, 2:]], axis=2)
    v_win = jnp.concatenate([vp[:, :-2], vp[:, 1:-1], vp[:, 2:]], axis=2)
    qpos = jnp.arange(w)[:, None]
    kidx = jnp.arange(3 * w)[None, :]
    rel = kidx - w - qpos
    kpos = jnp.arange(nb)[:, None, None] * w - w + kidx[None]
    mask = (jnp.abs(rel) <= WINDOW)[None] & (kpos >= 0) & (kpos < n_lat)
    s_win = jnp.einsum('bnqkgd,bnskd->bkgnqs', qb, k_win, preferred_element_type=jnp.float32) * scale
    s_win = jnp.where(mask, s_win, NEG_INF)
    s_ctx = jnp.einsum('bnqkgd,bskd->bkgnqs', qb, k_ctx.astype(q.dtype), preferred_element_type=jnp.float32) * scale
    s_sink = jnp.broadcast_to(sink.astype(jnp.float32).reshape(1, ATTN_KV_HEADS, ATTN_GROUP, 1, 1, 1),
                              s_win.shape[:-1] + (1,))
    p = jax.nn.softmax(jnp.concatenate([s_ctx, s_win, s_sink], axis=-1), axis=-1)
    p_ctx = p[..., :n_ctx].astype(v.dtype)
    p_win = p[..., n_ctx:n_ctx + 3 * w].astype(v.dtype)
    o = (jnp.einsum('bkgnqs,bskd->bnqkgd', p_ctx, v_ctx.astype(v.dtype))
         + jnp.einsum('bkgnqs,bnskd->bnqkgd', p_win, v_win))
    return o.reshape(bsz, n_lat, ATTN_Q_HEADS * ATTN_HD)


def _ssd_scan(x, a, b, c, s0):
    bsz, n, h, p = x.shape
    q = SSD_CHUNK
    nc = n // q
    x = x.reshape(bsz, nc, q, h, p)
    b = b.reshape(bsz, nc, q, h, -1)
    c = c.reshape(bsz, nc, q, h, -1)
    a_cum = jnp.cumsum(a.reshape(bsz, nc, q, h).transpose(0, 3, 1, 2), axis=-1)
    lmat = _tril_decay(a_cum)
    y_diag = jnp.einsum('bclhn,bcshn,bhcls,bcshp->bclhp', c, b, lmat, x)
    decay_states = jnp.exp(a_cum[..., -1:] - a_cum)
    states = jnp.einsum('bclhn,bhcl,bclhp->bchpn', b, decay_states, x)
    chunk_decay = jnp.exp(a_cum[..., -1])

    def step(s, inp):
        st, dec = inp
        return s * dec[..., None, None] + st, s

    s_final, s_prev = lax.scan(step, s0, (jnp.moveaxis(states, 1, 0), jnp.moveaxis(chunk_decay, 2, 0)))
    s_prev = jnp.moveaxis(s_prev, 0, 1)
    y_off = jnp.einsum('bclhn,bchpn,bhcl->bclhp', c, s_prev, jnp.exp(a_cum))
    return (y_diag + y_off).reshape(bsz, n, h, p), s_final


def _ssd_mixer(u, conv_w, conv_b, a_log, dt_bias, d_skip, norm_w, s0_fwd, s0_bwd):
    f32 = jnp.float32
    bsz, n, _ = u.shape
    z, xbc, dt = _split(u, [SSD_INNER, SSD_XBC, 2 * SSD_HEADS])
    xbc = jax.nn.silu(_dwconv_centred(xbc, conv_w) + conv_b.astype(xbc.dtype)).astype(f32)
    xs, bm, cm = _split(xbc, [SSD_INNER, SSD_GROUPS * SSD_STATE, SSD_GROUPS * SSD_STATE])
    xs = xs.reshape(bsz, n, SSD_HEADS, SSD_HD)
    rep = SSD_HEADS // SSD_GROUPS
    bm = jnp.repeat(bm.reshape(bsz, n, SSD_GROUPS, SSD_STATE), rep, axis=2)
    cm = jnp.repeat(cm.reshape(bsz, n, SSD_GROUPS, SSD_STATE), rep, axis=2)
    dt = jax.nn.softplus(dt.astype(f32).reshape(bsz, n, 2, SSD_HEADS) + dt_bias.astype(f32))
    a = -jnp.exp(a_log.astype(f32)) * dt
    y_f, s_f = _ssd_scan(xs * dt[:, :, 0, :, None], a[:, :, 0], bm, cm, s0_fwd.astype(f32))
    y_b, s_b = _ssd_scan(_flip(xs * dt[:, :, 1, :, None]), _flip(a[:, :, 1]), _flip(bm), _flip(cm),
                         s0_bwd.astype(f32))
    y = y_f + _flip(y_b) + d_skip.astype(f32)[:, None] * xs
    y = y.reshape(bsz, n, SSD_INNER) * jax.nn.silu(z.astype(f32))
    return _rmsnorm(y, norm_w).astype(u.dtype), s_f, s_b


def _gated_delta_scan(q, k, v, beta, g, s0):
    bsz, n, h, _ = q.shape
    dv = v.shape[-1]
    cq = DN_CHUNK
    nc = n // cq

    def chunks(t):
        return t.reshape(bsz, nc, cq, h, -1).transpose(0, 3, 1, 2, 4)

    q, k, v = chunks(q), chunks(k), chunks(v)
    beta = chunks(beta[..., None])[..., 0]
    g_cum = jnp.cumsum(chunks(g[..., None])[..., 0], axis=-1)
    decay = _tril_decay(g_cum)
    k_beta = k * beta[..., None]
    strict = jnp.tril(jnp.ones((cq, cq), dtype=bool), -1)
    lower = jnp.where(strict, jnp.einsum('bhcid,bhcjd->bhcij', k_beta, k) * decay, 0.0)
    eye = jnp.eye(cq, dtype=jnp.float32)
    rhs = jnp.concatenate([v * beta[..., None], k_beta * jnp.exp(g_cum)[..., None]], axis=-1)
    sol = lax.linalg.triangular_solve(lower + eye, rhs, left_side=True, lower=True, unit_diagonal=True)
    u, w = sol[..., :dv], sol[..., dv:]
    attn = jnp.einsum('bhcid,bhcjd->bhcij', q, k) * decay

    def step(s, inp):
        qi, ki, ui, wi, gi, ai = inp
        v_new = ui - jnp.einsum('bhqk,bhkv->bhqv', wi, s)
        o = (jnp.einsum('bhqk,bhkv->bhqv', qi * jnp.exp(gi)[..., None], s)
             + jnp.einsum('bhqs,bhsv->bhqv', ai, v_new))
        g_last = gi[..., -1:]
        s = s * jnp.exp(g_last)[..., None] + jnp.einsum('bhqk,bhqv->bhkv', ki * jnp.exp(g_last - gi)[..., None], v_new)
        return s, o

    xs = tuple(jnp.moveaxis(t, 2, 0) for t in (q, k, u, w, g_cum, attn))
    s_final, o = lax.scan(step, s0, xs)
    o = o.transpose(1, 0, 3, 2, 4).reshape(bsz, n, h, dv)
    return o, s_final


def _dn_mixer(u, conv_w, a_log, dt_bias, norm_w, s0_fwd, s0_bwd):
    f32 = jnp.float32
    bsz, n, _ = u.shape
    qkv, z, bt, al = _split(u, [DN_QKV, DN_HEADS * DN_DV, 2 * DN_HEADS, 2 * DN_HEADS])
    qkv = jax.nn.silu(_dwconv_centred(qkv, conv_w)).astype(f32)
    q, k, v = _split(qkv, [DN_HEADS * DN_DK, DN_HEADS * DN_DK, DN_HEADS * DN_DV])
    q = _l2norm(q.reshape(bsz, n, DN_HEADS, DN_DK)) * (DN_DK ** -0.5)
    k = _l2norm(k.reshape(bsz, n, DN_HEADS, DN_DK))
    v = v.reshape(bsz, n, DN_HEADS, DN_DV)
    beta = jax.nn.sigmoid(bt.astype(f32).reshape(bsz, n, 2, DN_HEADS))
    g = -jnp.exp(a_log.astype(f32)) * jax.nn.softplus(al.astype(f32).reshape(bsz, n, 2, DN_HEADS)
                                                       + dt_bias.astype(f32))
    o_f, s_f = _gated_delta_scan(q, k, v, beta[:, :, 0], g[:, :, 0], s0_fwd.astype(f32))
    o_b, s_b = _gated_delta_scan(_flip(q), _flip(k), _flip(v), _flip(beta[:, :, 1]), _flip(g[:, :, 1]),
                                 s0_bwd.astype(f32))
    o = _rmsnorm(o_f + _flip(o_b), norm_w) * jax.nn.silu(z.astype(f32).reshape(bsz, n, DN_HEADS, DN_DV))
    return o.reshape(bsz, n, DN_HEADS * DN_DV).astype(u.dtype), s_f, s_b


def _mixer_block(h, mix_w, cache):
    (w_in, w_out, q_norm_w, k_norm_w, attn_sink, ssd_conv_w, ssd_conv_b, ssd_a_log, ssd_dt_bias,
     ssd_d, ssd_norm_w, dn_conv_w, dn_a_log, dn_dt_bias, dn_norm_w) = mix_w
    bsz, n, _ = h.shape
    proj = jnp.einsum('bld,dc->blc', h, w_in)
    pa, ps, pd = _split(proj, [ATTN_COLS, SSD_COLS, DN_COLS])
    q, k, v = _split(pa, [ATTN_Q_HEADS * ATTN_HD, ATTN_KV_HEADS * ATTN_HD, ATTN_KV_HEADS * ATTN_HD])
    q = _rmsnorm(q.reshape(bsz, n, ATTN_Q_HEADS, ATTN_HD), q_norm_w)
    k = _rmsnorm(k.reshape(bsz, n, ATTN_KV_HEADS, ATTN_HD), k_norm_w)
    v = v.reshape(bsz, n, ATTN_KV_HEADS, ATTN_HD)
    if cache is None:
        o_attn = _context_attention(q, k, v, attn_sink)
        ssd0 = jnp.zeros((bsz, SSD_HEADS, SSD_HD, SSD_STATE), jnp.float32)
        dn0 = jnp.zeros((bsz, DN_HEADS, DN_DK, DN_DV), jnp.float32)
        ssd_f0, ssd_b0, dn_f0, dn_b0 = ssd0, ssd0, dn0, dn0
    else:
        k_ctx, v_ctx, ssd_f0, ssd_b0, dn_f0, dn_b0 = cache
        row, col = _grid_positions(n)
        q = _rope_2d(q, row, col)
        k = _rope_2d(k, row, col)
        o_attn = _latent_attention(q, k, v, k_ctx, v_ctx, attn_sink)
    o_ssd, ssd_f, ssd_b = _ssd_mixer(ps, ssd_conv_w, ssd_conv_b, ssd_a_log, ssd_dt_bias, ssd_d, ssd_norm_w,
                                     ssd_f0, ssd_b0)
    o_dn, dn_f, dn_b = _dn_mixer(pd, dn_conv_w, dn_a_log, dn_dt_bias, dn_norm_w, dn_f0, dn_b0)
    mixed = jnp.concatenate([o_attn.astype(h.dtype), o_ssd, o_dn], axis=-1)
    out = jnp.einsum('blm,md->bld', mixed, w_out)
    return out, (k, v, ssd_f, ssd_b, dn_f, dn_b)


def _ec_moe(h, router_w, w_gate, w_up, w_down):
    bsz, n, d = h.shape
    t = h.reshape(bsz * n, d)
    cap = max(1, (EC_CAPACITY_FACTOR * bsz * n) // N_EXPERTS)
    aff = jax.nn.softmax(jnp.einsum('td,de->te', t, router_w, preferred_element_type=jnp.float32), axis=-1)
    gates, idx = lax.top_k(aff.T, cap)
    xe = t[idx]
    hid = (jax.nn.silu(jnp.einsum('ecd,edf->ecf', xe, w_gate))
           * jnp.einsum('ecd,edf->ecf', xe, w_up))
    ye = jnp.einsum('ecf,efd->ecd', hid, w_down) * gates[..., None].astype(hid.dtype)
    out = jnp.zeros_like(t).at[idx.reshape(-1)].add(ye.reshape(-1, d).astype(t.dtype))
    return out.reshape(bsz, n, d)


def _modulation(cvec, w_mod, b_mod):
    m = jnp.einsum('nd,dm->nm', jax.nn.silu(cvec), w_mod) + b_mod
    return tuple(part[:, None, :] for part in jnp.split(m, N_MOD, axis=-1))


def _layer(x, mod, norm1_w, norm2_w, mix_w, moe_w, cache):
    sh1, sc1, g1, sh2, sc2, g2 = mod
    h = _rmsnorm(x, norm1_w) * (1.0 + sc1) + sh1
    mix_out, ctx_state = _mixer_block(h, mix_w, cache)
    x = x + g1 * mix_out
    h = _rmsnorm(x, norm2_w) * (1.0 + sc2) + sh2
    x = x + g2 * _ec_moe(h, *moe_w)
    return x, ctx_state


def _inv_softplus_dt(key, shape):
    dt = jnp.exp(jax.random.uniform(key, shape, jnp.float32, math.log(1e-3), math.log(1e-1)))
    return dt + jnp.log(-jnp.expm1(-dt))


def setup_inputs(seed: int = 0) -> dict:
    key = jax.random.key(seed)
    ks = jax.random.split(key, 40)
    f32 = jnp.float32
    D = D_MODEL

    def nrm(k, shape, s):
        return jax.random.normal(k, shape, f32) * s

    return {
        'x_prompt': nrm(ks[0], (BATCH, SEQ, D), 1.0),
        'x_sample': nrm(ks[1], (DEC_BATCH, DEC_SEQ, D), 1.0),
        'c': nrm(ks[2], (DEC_BATCH, D), 1.0),
        'cache_attn_k': nrm(ks[3], (DEC_BATCH, DEPTH, PAST_LEN, ATTN_KV_HEADS, ATTN_HD), 1.0),
        'cache_attn_v': nrm(ks[4], (DEC_BATCH, DEPTH, PAST_LEN, ATTN_KV_HEADS, ATTN_HD), 1.0),
        'state_ssd_fwd': nrm(ks[5], (DEC_BATCH, DEPTH, SSD_HEADS, SSD_HD, SSD_STATE), 0.5),
        'state_ssd_bwd': nrm(ks[6], (DEC_BATCH, DEPTH, SSD_HEADS, SSD_HD, SSD_STATE), 0.5),
        'state_dn_fwd': nrm(ks[7], (DEC_BATCH, DEPTH, DN_HEADS, DN_DK, DN_DV), 0.3),
        'state_dn_bwd': nrm(ks[8], (DEC_BATCH, DEPTH, DN_HEADS, DN_DK, DN_DV), 0.3),
        'c_ctx': nrm(ks[9], (D,), 1.0),
        'w_mod': nrm(ks[10], (DEPTH, D, N_MOD * D), 0.5 * D ** -0.5),
        'b_mod': nrm(ks[11], (DEPTH, N_MOD * D), 0.02),
        'norm1_w': 1.0 + nrm(ks[12], (DEPTH, D), 0.02),
        'norm2_w': 1.0 + nrm(ks[13], (DEPTH, D), 0.02),
        'w_in': nrm(ks[14], (DEPTH, D, IN_COLS), D ** -0.5),
        'w_out': nrm(ks[15], (DEPTH, MIX_WIDTH, D), MIX_WIDTH ** -0.5),
        'q_norm_w': 1.0 + nrm(ks[16], (DEPTH, ATTN_HD), 0.02),
        'k_norm_w': 1.0 + nrm(ks[17], (DEPTH, ATTN_HD), 0.02),
        'attn_sink': nrm(ks[18], (DEPTH, ATTN_Q_HEADS), 0.5),
        'ssd_conv_w': nrm(ks[19], (DEPTH, CONV_K, SSD_XBC), CONV_K ** -0.5),
        'ssd_conv_b': nrm(ks[20], (DEPTH, SSD_XBC), 0.02),
        'ssd_a_log': jnp.log(jax.random.uniform(ks[21], (DEPTH, 2, SSD_HEADS), f32, 1.0, 16.0)),
        'ssd_dt_bias': _inv_softplus_dt(ks[22], (DEPTH, 2, SSD_HEADS)),
        'ssd_d': 1.0 + nrm(ks[23], (DEPTH, SSD_HEADS), 0.1),
        'ssd_norm_w': 1.0 + nrm(ks[24], (DEPTH, SSD_INNER), 0.02),
        'dn_conv_w': nrm(ks[25], (DEPTH, CONV_K, DN_QKV), CONV_K ** -0.5),
        'dn_a_log': jnp.log(jax.random.uniform(ks[26], (DEPTH, 2, DN_HEADS), f32, 1.0, 16.0)),
        'dn_dt_bias': _inv_softplus_dt(ks[27], (DEPTH, 2, DN_HEADS)),
        'dn_norm_w': 1.0 + nrm(ks[28], (DEPTH, DN_DV), 0.02),
        'router_w': nrm(ks[29], (DEPTH, D, N_EXPERTS), D ** -0.5),
        'w_gate': nrm(ks[30], (DEPTH, N_EXPERTS, D, EXPERT_FF), D ** -0.5),
        'w_up': nrm(ks[31], (DEPTH, N_EXPERTS, D, EXPERT_FF), D ** -0.5),
        'w_down': nrm(ks[32], (DEPTH, N_EXPERTS, EXPERT_FF, D), EXPERT_FF ** -0.5),
    }


def reference(x_prompt, x_sample, c, cache_attn_k, cache_attn_v, state_ssd_fwd, state_ssd_bwd,
              state_dn_fwd, state_dn_bwd, c_ctx, w_mod, b_mod, norm1_w, norm2_w, w_in, w_out,
              q_norm_w, k_norm_w, attn_sink, ssd_conv_w, ssd_conv_b, ssd_a_log, ssd_dt_bias, ssd_d,
              ssd_norm_w, dn_conv_w, dn_a_log, dn_dt_bias, dn_norm_w, router_w, w_gate, w_up, w_down):
    mix_ws = [(w_in[l], w_out[l], q_norm_w[l], k_norm_w[l], attn_sink[l], ssd_conv_w[l], ssd_conv_b[l],
               ssd_a_log[l], ssd_dt_bias[l], ssd_d[l], ssd_norm_w[l], dn_conv_w[l], dn_a_log[l],
               dn_dt_bias[l], dn_norm_w[l]) for l in range(DEPTH)]
    moe_ws = [(router_w[l], w_gate[l], w_up[l], w_down[l]) for l in range(DEPTH)]

    x = x_prompt
    ctx_states = []
    for l in range(DEPTH):
        mod = _modulation(c_ctx[None, :], w_mod[l], b_mod[l])
        x, st = _layer(x, mod, norm1_w[l], norm2_w[l], mix_ws[l], moe_ws[l], None)
        ctx_states.append(st)
    y_prompt = x

    x = x_sample
    for l in range(DEPTH):
        mod = _modulation(c, w_mod[l], b_mod[l])
        cache = (cache_attn_k[:, l], cache_attn_v[:, l], state_ssd_fwd[:, l], state_ssd_bwd[:, l],
                 state_dn_fwd[:, l], state_dn_bwd[:, l])
        x, _ = _layer(x, mod, norm1_w[l], norm2_w[l], mix_ws[l], moe_ws[l], cache)
    y_sample = x

    new_attn_k = jnp.stack([st[0] for st in ctx_states], axis=1)
    new_attn_v = jnp.stack([st[1] for st in ctx_states], axis=1)
    new_ssd_fwd = jnp.stack([st[2] for st in ctx_states], axis=1)
    new_ssd_bwd = jnp.stack([st[3] for st in ctx_states], axis=1)
    new_dn_fwd = jnp.stack([st[4] for st in ctx_states], axis=1)
    new_dn_bwd = jnp.stack([st[5] for st in ctx_states], axis=1)
    return (y_prompt, y_sample, new_attn_k, new_attn_v, new_ssd_fwd, new_ssd_bwd, new_dn_fwd, new_dn_bwd)
```

```python
import functools
import math

import jax
import jax.numpy as jnp
import numpy as np
from jax import lax
from jax.experimental import pallas as pl
from jax.experimental.pallas import tpu as pltpu

D_MODEL = 2048
BATCH = 16
SEQ = 256
DEPTH = 4
DEC_BATCH = 2
DEC_SEQ = 2048
PAST_LEN = 512
GRID_W = 64
N_MOD = 6
EPS = 1e-6
NEG_INF = -1e30

ATTN_Q_HEADS = 8
ATTN_KV_HEADS = 2
ATTN_HD = 64
ATTN_GROUP = ATTN_Q_HEADS // ATTN_KV_HEADS
ATTN_BLOCK = 128
WINDOW = 128
ROPE_BASE = 10000.0

SSD_HEADS = 12
SSD_HD = 64
SSD_INNER = SSD_HEADS * SSD_HD
SSD_GROUPS = 2
SSD_STATE = 128
SSD_CHUNK = 128

DN_HEADS = 6
DN_DK = 128
DN_DV = 128
DN_CHUNK = 64

CONV_K = 5

N_EXPERTS = 16
EXPERT_FF = 1024
EC_CAPACITY_FACTOR = 2

ATTN_COLS = (ATTN_Q_HEADS + 2 * ATTN_KV_HEADS) * ATTN_HD
SSD_XBC = SSD_INNER + 2 * SSD_GROUPS * SSD_STATE
SSD_COLS = SSD_INNER + SSD_XBC + 2 * SSD_HEADS
DN_QKV = DN_HEADS * (2 * DN_DK + DN_DV)
DN_COLS = DN_QKV + DN_HEADS * DN_DV + 4 * DN_HEADS
IN_COLS = ATTN_COLS + SSD_COLS + DN_COLS
MIX_WIDTH = ATTN_Q_HEADS * ATTN_HD + SSD_INNER + DN_HEADS * DN_DV

LANE = 128
N_CTX_TOK = BATCH * SEQ
N_LAT_TOK = DEC_BATCH * DEC_SEQ
N_TOK = N_CTX_TOK + N_LAT_TOK
N_COND = 1 + DEC_BATCH
COND_ROWS = 8

AS_COLS = ATTN_COLS + SSD_COLS
AS_PAD = -(-AS_COLS // LANE) * LANE
DN_PAD = -(-DN_COLS // LANE) * LANE
PROJ_COLS = AS_PAD + DN_PAD

CAP = max(1, (EC_CAPACITY_FACTOR * N_CTX_TOK) // N_EXPERTS)
ROUTER_PAD = LANE
MOE_ROWS = 512

VMEM_LIMIT = 56 * 1024 * 1024

bf16 = jnp.bfloat16
f32 = jnp.float32


def _cond_row(tile, tm):
    n_ctx_tiles = N_CTX_TOK // tm
    per_batch = DEC_SEQ // tm
    return jnp.where(tile < n_ctx_tiles, 0, 1 + (tile - n_ctx_tiles) // per_batch)


def _mod_kernel(cond_ref, w_ref, b_ref, o_ref):
    cond = cond_ref[...]
    s = (cond * jax.nn.sigmoid(cond)).astype(bf16)
    w = w_ref[0].astype(bf16)
    o_ref[0] = jnp.dot(s, w, preferred_element_type=f32) + b_ref[0]


def _modulation_all(cond, w_mod, b_mod, tn=1024):
    n = N_MOD * D_MODEL
    return pl.pallas_call(
        _mod_kernel,
        grid=(DEPTH, n // tn),
        in_specs=[
            pl.BlockSpec((COND_ROWS, D_MODEL), lambda l, j: (0, 0)),
            pl.BlockSpec((1, D_MODEL, tn), lambda l, j: (l, 0, j)),
            pl.BlockSpec((1, 1, tn), lambda l, j: (l, 0, j)),
        ],
        out_specs=pl.BlockSpec((1, COND_ROWS, tn), lambda l, j: (l, 0, j)),
        out_shape=jax.ShapeDtypeStruct((DEPTH, COND_ROWS, n), f32),
        compiler_params=pltpu.CompilerParams(
            dimension_semantics=("arbitrary", "arbitrary"), vmem_limit_bytes=VMEM_LIMIT),
        name="modulation",
    )(cond, w_mod, b_mod.reshape(DEPTH, 1, n))


def _rms_modulate(x, nw, shift, scale):
    y = x * lax.rsqrt(jnp.mean(x * x, axis=-1, keepdims=True) + EPS)
    return (y * nw) * (1.0 + scale) + shift


def _in_proj_kernel(x_ref, nw_ref, mod_ref, w_ref, o_ref, h_scr):
    @pl.when(pl.program_id(1) == 0)
    def _():
        h = _rms_modulate(x_ref[...], nw_ref[...], mod_ref[0, 0:1, :], mod_ref[0, 1:2, :])
        h_scr[...] = h.astype(bf16)

    o_ref[...] = jnp.dot(h_scr[...], w_ref[...], preferred_element_type=f32)


def _in_proj(x, nw, mod, w, tm=512, tn=1024):
    return pl.pallas_call(
        _in_proj_kernel,
        grid=(N_TOK // tm, PROJ_COLS // tn),
        in_specs=[
            pl.BlockSpec((tm, D_MODEL), lambda i, j: (i, 0)),
            pl.BlockSpec((1, D_MODEL), lambda i, j: (0, 0)),
            pl.BlockSpec((1, N_MOD, D_MODEL), lambda i, j: (_cond_row(i, tm), 0, 0)),
            pl.BlockSpec((D_MODEL, tn), lambda i, j: (0, j)),
        ],
        out_specs=pl.BlockSpec((tm, tn), lambda i, j: (i, j)),
        out_shape=jax.ShapeDtypeStruct((N_TOK, PROJ_COLS), f32),
        scratch_shapes=[pltpu.VMEM((tm, D_MODEL), bf16)],
        compiler_params=pltpu.CompilerParams(
            dimension_semantics=("arbitrary", "arbitrary"), vmem_limit_bytes=VMEM_LIMIT),
        name="in_proj",
    )(x, nw, mod, w)


def _out_proj_kernel(mix_ref, x_ref, w_ref, nw_ref, mod_ref, rw_ref, xo_ref, h_ref, lg_ref):
    out = jnp.dot(mix_ref[...].astype(bf16), w_ref[...], preferred_element_type=f32)
    xn = x_ref[...] + mod_ref[0, 2:3, :] * out
    xo_ref[...] = xn
    h = _rms_modulate(xn, nw_ref[...], mod_ref[0, 3:4, :], mod_ref[0, 4:5, :])
    h_ref[...] = h.astype(bf16)
    lg_ref[...] = jnp.dot(h, rw_ref[...], preferred_element_type=f32,
                          precision=lax.Precision.HIGHEST)


def _out_proj(mixed, x, w, nw, mod, rw, tm=256):
    return pl.pallas_call(
        _out_proj_kernel,
        grid=(N_TOK // tm,),
        in_specs=[
            pl.BlockSpec((tm, MIX_WIDTH), lambda i: (i, 0)),
            pl.BlockSpec((tm, D_MODEL), lambda i: (i, 0)),
            pl.BlockSpec((MIX_WIDTH, D_MODEL), lambda i: (0, 0)),
            pl.BlockSpec((1, D_MODEL), lambda i: (0, 0)),
            pl.BlockSpec((1, N_MOD, D_MODEL), lambda i: (_cond_row(i, tm), 0, 0)),
            pl.BlockSpec((D_MODEL, ROUTER_PAD), lambda i: (0, 0)),
        ],
        out_specs=[
            pl.BlockSpec((tm, D_MODEL), lambda i: (i, 0)),
            pl.BlockSpec((tm, D_MODEL), lambda i: (i, 0)),
            pl.BlockSpec((tm, ROUTER_PAD), lambda i: (i, 0)),
        ],
        out_shape=[
            jax.ShapeDtypeStruct((N_TOK, D_MODEL), f32),
            jax.ShapeDtypeStruct((N_TOK, D_MODEL), bf16),
            jax.ShapeDtypeStruct((N_TOK, ROUTER_PAD), f32),
        ],
        compiler_params=pltpu.CompilerParams(
            dimension_semantics=("arbitrary",), vmem_limit_bytes=VMEM_LIMIT),
        name="out_proj",
    )(mixed, x, w, nw, mod, rw)


def _moe_ffn_kernel(xe_ref, g_ref, wg_ref, wu_ref, wd_ref, o_ref):
    f = pl.program_id(1)
    last = pl.num_programs(1) - 1
    wg = wg_ref[0].astype(bf16)
    wu = wu_ref[0].astype(bf16)
    wd = wd_ref[0].astype(bf16)
    for r in range(xe_ref.shape[1] // MOE_ROWS):
        rows = pl.ds(r * MOE_ROWS, MOE_ROWS)
        xe = xe_ref[0, rows, :]
        gate = jnp.dot(xe, wg, preferred_element_type=f32)
        up = jnp.dot(xe, wu, preferred_element_type=f32)
        hid = (gate * jax.nn.sigmoid(gate)) * up
        part = jnp.dot(hid.astype(bf16), wd, preferred_element_type=f32)

        @pl.when(f == 0)
        def _():
            o_ref[0, rows, :] = part

        @pl.when(jnp.logical_and(f > 0, f < last))
        def _():
            o_ref[0, rows, :] += part

        @pl.when(f == last)
        def _():
            o_ref[0, rows, :] = (o_ref[0, rows, :] + part) * g_ref[0, rows, :]


def _moe_ffn(xe, gates, w_gate, w_up, w_down, layer, tf=256):
    m = 2 * CAP
    return pl.pallas_call(
        _moe_ffn_kernel,
        grid=(N_EXPERTS, EXPERT_FF // tf),
        in_specs=[
            pl.BlockSpec((1, m, D_MODEL), lambda e, f: (e, 0, 0)),
            pl.BlockSpec((1, m, 1), lambda e, f: (e, 0, 0)),
            pl.BlockSpec((1, D_MODEL, tf), lambda e, f: (layer * N_EXPERTS + e, 0, f)),
            pl.BlockSpec((1, D_MODEL, tf), lambda e, f: (layer * N_EXPERTS + e, 0, f)),
            pl.BlockSpec((1, tf, D_MODEL), lambda e, f: (layer * N_EXPERTS + e, f, 0)),
        ],
        out_specs=pl.BlockSpec((1, m, D_MODEL), lambda e, f: (e, 0, 0)),
        out_shape=jax.ShapeDtypeStruct((N_EXPERTS, m, D_MODEL), f32),
        compiler_params=pltpu.CompilerParams(
            dimension_semantics=("arbitrary", "arbitrary"), vmem_limit_bytes=VMEM_LIMIT),
        name="moe_ffn",
    )(xe, gates, w_gate, w_up, w_down)


def _split(a, widths):
    cuts = [int(v) for v in np.cumsum(widths)[:-1]]
    return jnp.split(a, cuts, axis=-1)


def _rmsnorm(x, w):
    xf = x.astype(f32)
    y = xf * lax.rsqrt(jnp.mean(xf * xf, axis=-1, keepdims=True) + EPS)
    return (y * w.astype(f32)).astype(x.dtype)


def _l2norm(x):
    return x * lax.rsqrt(jnp.sum(x * x, axis=-1, keepdims=True) + EPS)


def _flip(t):
    return jnp.flip(t, axis=1)


def _dwconv_centred(x, w):
    return lax.conv_general_dilated(
        x, w[:, None, :].astype(x.dtype), window_strides=(1,),
        padding=[(CONV_K // 2, CONV_K // 2)],
        dimension_numbers=('NWC', 'WIO', 'NWC'),
        feature_group_count=x.shape[-1])


def _grid_positions(n_tokens):
    rows = n_tokens // GRID_W
    r, col = jnp.meshgrid(jnp.arange(rows), jnp.arange(GRID_W), indexing='ij')
    return r.reshape(-1), col.reshape(-1)


def _rope_1d(x, pos):
    half = x.shape[-1] // 2
    inv = ROPE_BASE ** (-jnp.arange(half, dtype=f32) / half)
    ang = pos.astype(f32)[:, None] * inv[None, :]
    cos = jnp.cos(ang)[:, None, :]
    sin = jnp.sin(ang)[:, None, :]
    x1 = x[..., :half].astype(f32)
    x2 = x[..., half:].astype(f32)
    return jnp.concatenate([x1 * cos - x2 * sin, x2 * cos + x1 * sin], axis=-1)


def _rope_2d(x, row, col):
    h = x.shape[-1] // 2
    return jnp.concatenate([_rope_1d(x[..., :h], row), _rope_1d(x[..., h:], col)], axis=-1).astype(x.dtype)


def _tril_decay(a_cum):
    q = a_cum.shape[-1]
    tril = jnp.tril(jnp.ones((q, q), dtype=bool))
    diff = a_cum[..., :, None] - a_cum[..., None, :]
    return jnp.where(tril, jnp.exp(jnp.where(tril, diff, 0.0)), 0.0)


def _context_attention(q, k, v, sink):
    bsz, n_ctx = q.shape[:2]
    nqb = n_ctx // ATTN_BLOCK
    scale = ATTN_HD ** -0.5
    qb = jnp.moveaxis(q.reshape(bsz, nqb, ATTN_BLOCK, ATTN_KV_HEADS, ATTN_GROUP, ATTN_HD), 1, 0)
    sink_logit = jnp.broadcast_to(sink.astype(f32).reshape(1, ATTN_KV_HEADS, ATTN_GROUP, 1, 1),
                                  (bsz, ATTN_KV_HEADS, ATTN_GROUP, ATTN_BLOCK, 1))

    def one_block(qi):
        s = jnp.einsum('bqkgd,bskd->bkgqs', qi, k, preferred_element_type=f32) * scale
        p = jax.nn.softmax(jnp.concatenate([s, sink_logit], axis=-1), axis=-1)[..., :-1]
        return jnp.einsum('bkgqs,bskd->bqkgd', p.astype(v.dtype), v)

    o = lax.map(one_block, qb)
    return jnp.moveaxis(o, 0, 1).reshape(bsz, n_ctx, ATTN_Q_HEADS * ATTN_HD)


def _latent_attention(q, k, v, k_ctx, v_ctx, sink):
    bsz, n_lat = q.shape[:2]
    n_ctx = k_ctx.shape[1]
    w = ATTN_BLOCK
    nb = n_lat // w
    scale = ATTN_HD ** -0.5
    qb = q.reshape(bsz, nb, w, ATTN_KV_HEADS, ATTN_GROUP, ATTN_HD)
    pad = ((0, 0), (w, w), (0, 0), (0, 0))
    kp = jnp.pad(k, pad).reshape(bsz, nb + 2, w, ATTN_KV_HEADS, ATTN_HD)
    vp = jnp.pad(v, pad).reshape(bsz, nb + 2, w, ATTN_KV_HEADS, ATTN_HD)
    k_win = jnp.concatenate([kp[:, :-2], kp[:, 1:-1], kp[:, 2:]], axis=2)
    v_win = jnp.concatenate([vp[:, :-2], vp[:, 1:-1], vp[:, 2:]], axis=2)
    qpos = jnp.arange(w)[:, None]
    kidx = jnp.arange(3 * w)[None, :]
    rel = kidx - w - qpos
    kpos = jnp.arange(nb)[:, None, None] * w - w + kidx[None]
    mask = (jnp.abs(rel) <= WINDOW)[None] & (kpos >= 0) & (kpos < n_lat)
    s_win = jnp.einsum('bnqkgd,bnskd->bkgnqs', qb, k_win, preferred_element_type=f32) * scale
    s_win = jnp.where(mask, s_win, NEG_INF)
    s_ctx = jnp.einsum('bnqkgd,bskd->bkgnqs', qb, k_ctx.astype(q.dtype), preferred_element_type=f32) * scale
    s_sink = jnp.broadcast_to(sink.astype(f32).reshape(1, ATTN_KV_HEADS, ATTN_GROUP, 1, 1, 1),
                              s_win.shape[:-1] + (1,))
    p = jax.nn.softmax(jnp.concatenate([s_ctx, s_win, s_sink], axis=-1), axis=-1)
    p_ctx = p[..., :n_ctx].astype(v.dtype)
    p_win = p[..., n_ctx:n_ctx + 3 * w].astype(v.dtype)
    o = (jnp.einsum('bkgnqs,bskd->bnqkgd', p_ctx, v_ctx.astype(v.dtype))
         + jnp.einsum('bkgnqs,bnskd->bnqkgd', p_win, v_win))
    return o.reshape(bsz, n_lat, ATTN_Q_HEADS * ATTN_HD)


def _ssd_scan(x, a, b, c, s0):
    bsz, n, h, p = x.shape
    q = SSD_CHUNK
    nc = n // q
    x = x.reshape(bsz, nc, q, h, p)
    b = b.reshape(bsz, nc, q, h, -1)
    c = c.reshape(bsz, nc, q, h, -1)
    a_cum = jnp.cumsum(a.reshape(bsz, nc, q, h).transpose(0, 3, 1, 2), axis=-1)
    lmat = _tril_decay(a_cum)
    y_diag = jnp.einsum('bclhn,bcshn,bhcls,bcshp->bclhp', c, b, lmat, x)
    decay_states = jnp.exp(a_cum[..., -1:] - a_cum)
    states = jnp.einsum('bclhn,bhcl,bclhp->bchpn', b, decay_states, x)
    chunk_decay = jnp.exp(a_cum[..., -1])

    def step(s, inp):
        st, dec = inp
        return s * dec[..., None, None] + st, s

    s_final, s_prev = lax.scan(step, s0, (jnp.moveaxis(states, 1, 0), jnp.moveaxis(chunk_decay, 2, 0)))
    s_prev = jnp.moveaxis(s_prev, 0, 1)
    y_off = jnp.einsum('bclhn,bchpn,bhcl->bclhp', c, s_prev, jnp.exp(a_cum))
    return (y_diag + y_off).reshape(bsz, n, h, p), s_final


def _ssd_mixer(u, conv_w, conv_b, a_log, dt_bias, d_skip, norm_w, s0_fwd, s0_bwd):
    bsz, n, _ = u.shape
    z, xbc, dt = _split(u, [SSD_INNER, SSD_XBC, 2 * SSD_HEADS])
    xbc = jax.nn.silu(_dwconv_centred(xbc, conv_w) + conv_b.astype(xbc.dtype)).astype(f32)
    xs, bm, cm = _split(xbc, [SSD_INNER, SSD_GROUPS * SSD_STATE, SSD_GROUPS * SSD_STATE])
    xs = xs.reshape(bsz, n, SSD_HEADS, SSD_HD)
    rep = SSD_HEADS // SSD_GROUPS
    bm = jnp.repeat(bm.reshape(bsz, n, SSD_GROUPS, SSD_STATE), rep, axis=2)
    cm = jnp.repeat(cm.reshape(bsz, n, SSD_GROUPS, SSD_STATE), rep, axis=2)
    dt = jax.nn.softplus(dt.astype(f32).reshape(bsz, n, 2, SSD_HEADS) + dt_bias.astype(f32))
    a = -jnp.exp(a_log.astype(f32)) * dt
    y_f, s_f = _ssd_scan(xs * dt[:, :, 0, :, None], a[:, :, 0], bm, cm, s0_fwd.astype(f32))
    y_b, s_b = _ssd_scan(_flip(xs * dt[:, :, 1, :, None]), _flip(a[:, :, 1]), _flip(bm), _flip(cm),
                         s0_bwd.astype(f32))
    y = y_f + _flip(y_b) + d_skip.astype(f32)[:, None] * xs
    y = y.reshape(bsz, n, SSD_INNER) * jax.nn.silu(z.astype(f32))
    return _rmsnorm(y, norm_w).astype(u.dtype), s_f, s_b


def _gated_delta_scan(q, k, v, beta, g, s0):
    bsz, n, h, _ = q.shape
    dv = v.shape[-1]
    cq = DN_CHUNK
    nc = n // cq

    def chunks(t):
        return t.reshape(bsz, nc, cq, h, -1).transpose(0, 3, 1, 2, 4)

    q, k, v = chunks(q), chunks(k), chunks(v)
    beta = chunks(beta[..., None])[..., 0]
    g_cum = jnp.cumsum(chunks(g[..., None])[..., 0], axis=-1)
    decay = _tril_decay(g_cum)
    k_beta = k * beta[..., None]
    strict = jnp.tril(jnp.ones((cq, cq), dtype=bool), -1)
    lower = jnp.where(strict, jnp.einsum('bhcid,bhcjd->bhcij', k_beta, k) * decay, 0.0)
    eye = jnp.eye(cq, dtype=f32)
    rhs = jnp.concatenate([v * beta[..., None], k_beta * jnp.exp(g_cum)[..., None]], axis=-1)
    sol = lax.linalg.triangular_solve(lower + eye, rhs, left_side=True, lower=True, unit_diagonal=True)
    u, w = sol[..., :dv], sol[..., dv:]
    attn = jnp.einsum('bhcid,bhcjd->bhcij', q, k) * decay

    def step(s, inp):
        qi, ki, ui, wi, gi, ai = inp
        v_new = ui - jnp.einsum('bhqk,bhkv->bhqv', wi, s)
        o = (jnp.einsum('bhqk,bhkv->bhqv', qi * jnp.exp(gi)[..., None], s)
             + jnp.einsum('bhqs,bhsv->bhqv', ai, v_new))
        g_last = gi[..., -1:]
        s = s * jnp.exp(g_last)[..., None] + jnp.einsum('bhqk,bhqv->bhkv', ki * jnp.exp(g_last - gi)[..., None], v_new)
        return s, o

    xs = tuple(jnp.moveaxis(t, 2, 0) for t in (q, k, u, w, g_cum, attn))
    s_final, o = lax.scan(step, s0, xs)
    o = o.transpose(1, 0, 3, 2, 4).reshape(bsz, n, h, dv)
    return o, s_final


def _dn_mixer(u, conv_w, a_log, dt_bias, norm_w, s0_fwd, s0_bwd):
    bsz, n, _ = u.shape
    qkv, z, bt, al = _split(u, [DN_QKV, DN_HEADS * DN_DV, 2 * DN_HEADS, 2 * DN_HEADS])
    qkv = jax.nn.silu(_dwconv_centred(qkv, conv_w)).astype(f32)
    q, k, v = _split(qkv, [DN_HEADS * DN_DK, DN_HEADS * DN_DK, DN_HEADS * DN_DV])
    q = _l2norm(q.reshape(bsz, n, DN_HEADS, DN_DK)) * (DN_DK ** -0.5)
    k = _l2norm(k.reshape(bsz, n, DN_HEADS, DN_DK))
    v = v.reshape(bsz, n, DN_HEADS, DN_DV)
    beta = jax.nn.sigmoid(bt.astype(f32).reshape(bsz, n, 2, DN_HEADS))
    g = -jnp.exp(a_log.astype(f32)) * jax.nn.softplus(al.astype(f32).reshape(bsz, n, 2, DN_HEADS)
                                                      + dt_bias.astype(f32))
    o_f, s_f = _gated_delta_scan(q, k, v, beta[:, :, 0], g[:, :, 0], s0_fwd.astype(f32))
    o_b, s_b = _gated_delta_scan(_flip(q), _flip(k), _flip(v), _flip(beta[:, :, 1]), _flip(g[:, :, 1]),
                                 s0_bwd.astype(f32))
    o = _rmsnorm(o_f + _flip(o_b), norm_w) * jax.nn.silu(z.astype(f32).reshape(bsz, n, DN_HEADS, DN_DV))
    return o.reshape(bsz, n, DN_HEADS * DN_DV).astype(u.dtype), s_f, s_b


def _mixers(pa, ps, pd, mix_w, cache):
    (q_norm_w, k_norm_w, attn_sink, ssd_conv_w, ssd_conv_b, ssd_a_log, ssd_dt_bias,
     ssd_d, ssd_norm_w, dn_conv_w, dn_a_log, dn_dt_bias, dn_norm_w) = mix_w
    bsz, n, _ = pa.shape
    q, k, v = _split(pa, [ATTN_Q_HEADS * ATTN_HD, ATTN_KV_HEADS * ATTN_HD, ATTN_KV_HEADS * ATTN_HD])
    q = _rmsnorm(q.reshape(bsz, n, ATTN_Q_HEADS, ATTN_HD), q_norm_w)
    k = _rmsnorm(k.reshape(bsz, n, ATTN_KV_HEADS, ATTN_HD), k_norm_w)
    v = v.reshape(bsz, n, ATTN_KV_HEADS, ATTN_HD)
    if cache is None:
        o_attn = _context_attention(q, k, v, attn_sink)
        ssd0 = jnp.zeros((bsz, SSD_HEADS, SSD_HD, SSD_STATE), f32)
        dn0 = jnp.zeros((bsz, DN_HEADS, DN_DK, DN_DV), f32)
        ssd_f0, ssd_b0, dn_f0, dn_b0 = ssd0, ssd0, dn0, dn0
    else:
        k_ctx, v_ctx, ssd_f0, ssd_b0, dn_f0, dn_b0 = cache
        row, col = _grid_positions(n)
        q = _rope_2d(q, row, col)
        k = _rope_2d(k, row, col)
        o_attn = _latent_attention(q, k, v, k_ctx, v_ctx, attn_sink)
    o_ssd, ssd_f, ssd_b = _ssd_mixer(ps, ssd_conv_w, ssd_conv_b, ssd_a_log, ssd_dt_bias, ssd_d, ssd_norm_w,
                                     ssd_f0, ssd_b0)
    o_dn, dn_f, dn_b = _dn_mixer(pd, dn_conv_w, dn_a_log, dn_dt_bias, dn_norm_w, dn_f0, dn_b0)
    mixed = jnp.concatenate([o_attn, o_ssd, o_dn], axis=-1)
    return mixed, (k, v, ssd_f, ssd_b, dn_f, dn_b)


def _route(logits, h):
    aff = jax.nn.softmax(logits[:, :N_EXPERTS], axis=-1).reshape(2, N_CTX_TOK, N_EXPERTS)
    gates, idx = lax.top_k(jnp.swapaxes(aff, 1, 2), CAP)
    idx = idx + jnp.arange(2, dtype=idx.dtype)[:, None, None] * N_CTX_TOK
    idx = jnp.swapaxes(idx, 0, 1).reshape(N_EXPERTS, 2 * CAP)
    gates = jnp.swapaxes(gates, 0, 1).reshape(N_EXPERTS, 2 * CAP, 1)
    return h[idx], gates, idx


def kernel(x_prompt, x_sample, c, cache_attn_k, cache_attn_v, state_ssd_fwd, state_ssd_bwd, state_dn_fwd, state_dn_bwd, c_ctx, w_mod, b_mod, norm1_w, norm2_w, w_in, w_out, q_norm_w, k_norm_w, attn_sink, ssd_conv_w, ssd_conv_b, ssd_a_log, ssd_dt_bias, ssd_d, ssd_norm_w, dn_conv_w, dn_a_log, dn_dt_bias, dn_norm_w, router_w, w_gate, w_up, w_down):
    cond = jnp.concatenate([c_ctx[None, :], c, jnp.zeros((COND_ROWS - N_COND, D_MODEL), f32)], axis=0)
    mod = _modulation_all(cond, w_mod, b_mod).reshape(DEPTH, COND_ROWS, N_MOD, D_MODEL)

    w_in_p = jnp.concatenate([
        w_in[:, :, :AS_COLS].astype(bf16), jnp.zeros((DEPTH, D_MODEL, AS_PAD - AS_COLS), bf16),
        w_in[:, :, AS_COLS:].astype(bf16), jnp.zeros((DEPTH, D_MODEL, DN_PAD - DN_COLS), bf16)], axis=-1)
    w_out_b = w_out.astype(bf16)
    rw_p = jnp.pad(router_w, ((0, 0), (0, 0), (0, ROUTER_PAD - N_EXPERTS)))
    wg = w_gate.reshape(DEPTH * N_EXPERTS, D_MODEL, EXPERT_FF)
    wu = w_up.reshape(DEPTH * N_EXPERTS, D_MODEL, EXPERT_FF)
    wd = w_down.reshape(DEPTH * N_EXPERTS, EXPERT_FF, D_MODEL)

    x = jnp.concatenate([x_prompt.reshape(N_CTX_TOK, D_MODEL), x_sample.reshape(N_LAT_TOK, D_MODEL)], axis=0)
    ctx_states = []
    for l in range(DEPTH):
        proj = _in_proj(x, norm1_w[l][None, :], mod[l], w_in_p[l])
        mix_w = (q_norm_w[l], k_norm_w[l], attn_sink[l], ssd_conv_w[l], ssd_conv_b[l], ssd_a_log[l],
                 ssd_dt_bias[l], ssd_d[l], ssd_norm_w[l], dn_conv_w[l], dn_a_log[l], dn_dt_bias[l], dn_norm_w[l])
        pc = proj[:N_CTX_TOK].reshape(BATCH, SEQ, PROJ_COLS)
        pl_ = proj[N_CTX_TOK:].reshape(DEC_BATCH, DEC_SEQ, PROJ_COLS)
        cache = (cache_attn_k[:, l], cache_attn_v[:, l], state_ssd_fwd[:, l], state_ssd_bwd[:, l],
                 state_dn_fwd[:, l], state_dn_bwd[:, l])
        mixed_c, st = _mixers(pc[..., :ATTN_COLS], pc[..., ATTN_COLS:AS_COLS], pc[..., AS_PAD:AS_PAD + DN_COLS],
                              mix_w, None)
        mixed_l, _ = _mixers(pl_[..., :ATTN_COLS], pl_[..., ATTN_COLS:AS_COLS], pl_[..., AS_PAD:AS_PAD + DN_COLS],
                             mix_w, cache)
        ctx_states.append(st)
        mixed = jnp.concatenate([mixed_c.reshape(N_CTX_TOK, MIX_WIDTH), mixed_l.reshape(N_LAT_TOK, MIX_WIDTH)], axis=0)
        x, h2, logits = _out_proj(mixed, x, w_out_b[l], norm2_w[l][None, :], mod[l], rw_p[l])
        xe, gates, idx = _route(logits, h2)
        ye = _moe_ffn(xe, gates, wg, wu, wd, l)
        moe = jnp.zeros((N_TOK, D_MODEL), f32).at[idx.reshape(-1)].add(ye.reshape(-1, D_MODEL))
        g2 = jnp.repeat(mod[l, :N_COND, 5, :], np.array([N_CTX_TOK] + [DEC_SEQ] * DEC_BATCH), axis=0,
                        total_repeat_length=N_TOK)
        x = x + g2 * moe

    y_prompt = x[:N_CTX_TOK].reshape(BATCH, SEQ, D_MODEL)
    y_sample = x[N_CTX_TOK:].reshape(DEC_BATCH, DEC_SEQ, D_MODEL)
    new_attn_k = jnp.stack([st[0] for st in ctx_states], axis=1)
    new_attn_v = jnp.stack([st[1] for st in ctx_states], axis=1)
    new_ssd_fwd = jnp.stack([st[2] for st in ctx_states], axis=1)
    new_ssd_bwd = jnp.stack([st[3] for st in ctx_states], axis=1)
    new_dn_fwd = jnp.stack([st[4] for st in ctx_states], axis=1)
    new_dn_bwd = jnp.stack([st[5] for st in ctx_states], axis=1)
    return (y_prompt, y_sample, new_attn_k, new_attn_v, new_ssd_fwd, new_ssd_bwd, new_dn_fwd, new_dn_bwd)
```
